```python
import jax, jax.numpy as jnp
from jax import lax
import numpy as np

D_MODEL = 1024
BATCH = 4
SEQ = 8192
DEPTH = 4

N_MIXERS = 2
EXPAND = 2
D_INNER = EXPAND * D_MODEL
CONV_WIDTH = 31
CONV_IN_PROJ = 3 * D_INNER
RET_HEADS = 4
RET_QK_DIM = D_MODEL // RET_HEADS
RET_V_DIM = D_INNER // RET_HEADS
RET_IN_PROJ = 2 * RET_HEADS * RET_QK_DIM + 2 * D_INNER
CHUNK = 128
ROPE_BASE = 10000.0
EPS = 1e-6
N_CONV_LAYERS = (DEPTH + 1) // 2
N_RET_LAYERS = DEPTH // 2

kernel_name = "hybrid_conformer_conv_retention_trunk"


def rms_norm(x, g):
    xf = x.astype(jnp.float32)
    y = xf * lax.rsqrt(jnp.mean(xf * xf, axis=-1, keepdims=True) + EPS)
    return (y * g.astype(jnp.float32)).astype(x.dtype)


def conv_module(h, w_in, dw_w, dw_b, ln_g, ln_b, w_out):
    proj = h @ w_in
    a, b, z = jnp.split(proj, 3, axis=-1)
    u = a * jax.nn.sigmoid(b)
    c = lax.conv_general_dilated(
        u, dw_w[:, None, :].astype(u.dtype), window_strides=(1,),
        padding=((CONV_WIDTH - 1, 0),),
        dimension_numbers=("NWC", "WIO", "NWC"),
        feature_group_count=D_INNER) + dw_b
    cf = c.astype(jnp.float32)
    mu = jnp.mean(cf, axis=-1, keepdims=True)
    var = jnp.mean(jnp.square(cf - mu), axis=-1, keepdims=True)
    cn = ((cf - mu) * lax.rsqrt(var + EPS) * ln_g.astype(jnp.float32) + ln_b.astype(jnp.float32))
    y = jax.nn.silu(cn).astype(h.dtype) * jax.nn.silu(z)
    return y @ w_out


def apply_rotary(t, cos, sin):
    t1, t2 = jnp.split(t.astype(jnp.float32), 2, axis=-1)
    return jnp.concatenate([t1 * cos - t2 * sin, t1 * sin + t2 * cos], axis=-1).astype(t.dtype)


def retention_module(h, positions, w_in, w_out):
    bsz, seq, _ = h.shape
    nc = seq // CHUNK
    qk_w = RET_HEADS * RET_QK_DIM
    proj = h @ w_in
    q, k, v, g = jnp.split(proj, [qk_w, 2 * qk_w, 2 * qk_w + D_INNER], axis=-1)
    q = q.reshape(bsz, seq, RET_HEADS, RET_QK_DIM)
    k = k.reshape(bsz, seq, RET_HEADS, RET_QK_DIM)
    v = v.reshape(bsz, seq, RET_HEADS, RET_V_DIM)

    inv_freq = ROPE_BASE ** (-jnp.arange(RET_QK_DIM // 2, dtype=jnp.float32) / (RET_QK_DIM // 2))
    ang = positions.astype(jnp.float32)[..., None] * inv_freq
    cos, sin = jnp.cos(ang)[:, :, None, :], jnp.sin(ang)[:, :, None, :]
    q = apply_rotary(q, cos, sin)
    k = apply_rotary(k, cos, sin) * (RET_QK_DIM ** -0.5)

    qc = q.reshape(bsz, nc, CHUNK, RET_HEADS, RET_QK_DIM)
    kc = k.reshape(bsz, nc, CHUNK, RET_HEADS, RET_QK_DIM)
    vc = v.reshape(bsz, nc, CHUNK, RET_HEADS, RET_V_DIM)

    log_gamma = jnp.log(1.0 - 2.0 ** (-5.0 - jnp.arange(RET_HEADS, dtype=jnp.float32)))
    idx = jnp.arange(CHUNK, dtype=jnp.float32)
    diff = idx[:, None] - idx[None, :]
    causal = diff >= 0
    inner_decay = jnp.where(causal[None], jnp.exp(jnp.where(causal, diff, 0.0)[None] * log_gamma[:, None, None]), 0.0)
    q_decay = jnp.exp((idx + 1.0)[:, None] * log_gamma)[..., None]
    k_decay = jnp.exp((CHUNK - 1.0 - idx)[:, None] * log_gamma)[..., None]
    chunk_decay = jnp.exp(CHUNK * log_gamma)[:, None, None]

    scores = jnp.einsum("bnihd,bnjhd->bnhij", qc, kc) * inner_decay
    inner = jnp.einsum("bnhij,bnjhe->bnihe", scores, vc)

    def step(state, xs):
        qn, kn, vn = xs
        cross = jnp.einsum("bihd,bhde->bihe", qn * q_decay, state)
        state = state * chunk_decay + jnp.einsum("bjhd,bjhe->bhde", kn * k_decay, vn)
        return state, cross

    state0 = jnp.zeros((bsz, RET_HEADS, RET_QK_DIM, RET_V_DIM), jnp.float32)
    _, cross = lax.scan(step, state0, (jnp.moveaxis(qc, 1, 0), jnp.moveaxis(kc, 1, 0), jnp.moveaxis(vc, 1, 0)))
    o = (inner + jnp.moveaxis(cross, 0, 1)).astype(jnp.float32).reshape(bsz, seq, RET_HEADS, RET_V_DIM)

    o = o * lax.rsqrt(jnp.mean(o * o, axis=-1, keepdims=True) + EPS)
    o = o.reshape(bsz, seq, D_INNER).astype(h.dtype)
    return (jax.nn.silu(g) * o) @ w_out


def setup_inputs(seed: int = 0) -> dict:
    key = jax.random.key(seed)
    ks = jax.random.split(key, 14)
    f32 = jnp.float32
    x = jax.random.normal(ks[0], (BATCH, SEQ, D_MODEL), f32)
    positions = jnp.broadcast_to(jnp.arange(SEQ, dtype=jnp.int32), (BATCH, SEQ))
    conv_norm = 1.0 + 0.01 * jax.random.normal(ks[1], (N_CONV_LAYERS, D_MODEL), f32)
    conv_w_in = jax.random.normal(ks[2], (N_CONV_LAYERS, D_MODEL, CONV_IN_PROJ), f32) * D_MODEL ** -0.5
    conv_dw_w = jax.random.normal(ks[3], (N_CONV_LAYERS, CONV_WIDTH, D_INNER), f32) * CONV_WIDTH ** -0.5
    conv_dw_b = 0.01 * jax.random.normal(ks[4], (N_CONV_LAYERS, D_INNER), f32)
    conv_ln_g = 1.0 + 0.01 * jax.random.normal(ks[5], (N_CONV_LAYERS, D_INNER), f32)
    conv_ln_b = 0.01 * jax.random.normal(ks[6], (N_CONV_LAYERS, D_INNER), f32)
    conv_w_out = jax.random.normal(ks[7], (N_CONV_LAYERS, D_INNER, D_MODEL), f32) * (0.5 * D_INNER ** -0.5)
    ret_norm = 1.0 + 0.01 * jax.random.normal(ks[8], (N_RET_LAYERS, D_MODEL), f32)
    ret_w_in = jax.random.normal(ks[9], (N_RET_LAYERS, D_MODEL, RET_IN_PROJ), f32) * D_MODEL ** -0.5
    ret_w_out = jax.random.normal(ks[10], (N_RET_LAYERS, D_INNER, D_MODEL), f32) * (0.5 * D_INNER ** -0.5)
    final_norm = 1.0 + 0.01 * jax.random.normal(ks[11], (D_MODEL,), f32)
    return {"x": x, "positions": positions,
            "conv_norm": conv_norm, "conv_w_in": conv_w_in, "conv_dw_w": conv_dw_w, "conv_dw_b": conv_dw_b,
            "conv_ln_g": conv_ln_g, "conv_ln_b": conv_ln_b, "conv_w_out": conv_w_out,
            "ret_norm": ret_norm, "ret_w_in": ret_w_in, "ret_w_out": ret_w_out,
            "final_norm": final_norm}


def reference(x, positions, conv_norm, conv_w_in, conv_dw_w, conv_dw_b, conv_ln_g, conv_ln_b, conv_w_out,
              ret_norm, ret_w_in, ret_w_out, final_norm):
    for i in range(DEPTH):
        j = i // N_MIXERS
        if i % N_MIXERS == 0:
            h = rms_norm(x, conv_norm[j])
            y = conv_module(h, conv_w_in[j], conv_dw_w[j], conv_dw_b[j], conv_ln_g[j], conv_ln_b[j], conv_w_out[j])
        else:
            h = rms_norm(x, ret_norm[j])
            y = retention_module(h, positions, ret_w_in[j], ret_w_out[j])
        x = x + y.astype(x.dtype)
    return rms_norm(x, final_norm)
```

```python
import functools

import jax
import jax.numpy as jnp
from jax import lax
from jax.experimental import pallas as pl
from jax.experimental.pallas import tpu as pltpu

D_MODEL = 1024
D_INNER = 2048
CONV_WIDTH = 31
HEADS = 4
QK_DIM = D_MODEL // HEADS
V_DIM = D_INNER // HEADS
QK_ALL = HEADS * QK_DIM
CHUNK = 128
ROPE_BASE = 10000.0
EPS = 1e-6
DEPTH = 4

LANES = 128
SUBLANES = 8
SEQ_TILE = 512
HALO = 32
CONV_ROWS = 64
PROJ_COLS = 512
VMEM_LIMIT = 56 * 1024 * 1024

_F32 = jnp.float32
_BF16 = jnp.bfloat16


def _dot(a, b):
    return jnp.dot(a, b, preferred_element_type=_F32)


def _sigmoid(x):
    return 1.0 / (1.0 + jnp.exp(-x))


def _rms_norm_rows(x, w):
    return x * lax.rsqrt(jnp.mean(x * x, axis=-1, keepdims=True) + EPS) * w


def _conv_layer_kernel(x_ref, nw_ref, win_ref, dww_ref, dwb_ref, lng_ref, lnb_ref, wout_ref,
                       o_ref, hb_ref, ubuf_ref, carry_ref, c_ref, z_ref):
    tile = x_ref.shape[1]
    n_slabs = D_INNER // LANES

    @pl.when(pl.program_id(1) == 0)
    def _():
        carry_ref[...] = jnp.zeros_like(carry_ref)

    x = x_ref[0]
    hb_ref[...] = _rms_norm_rows(x, nw_ref[...]).astype(_BF16)

    for j in range(D_INNER // PROJ_COLS):
        c0 = j * PROJ_COLS
        hb = hb_ref[...]
        a = _dot(hb, win_ref[:, c0:c0 + PROJ_COLS])
        b = _dot(hb, win_ref[:, D_INNER + c0:D_INNER + c0 + PROJ_COLS])
        z = _dot(hb, win_ref[:, 2 * D_INNER + c0:2 * D_INNER + c0 + PROJ_COLS])
        u = a * _sigmoid(b)
        z_ref[:, c0:c0 + PROJ_COLS] = z * _sigmoid(z)
        for i in range(PROJ_COLS // LANES):
            slab = j * (PROJ_COLS // LANES) + i
            ubuf_ref[slab, 0:HALO, :] = carry_ref[slab]
            ubuf_ref[slab, HALO:HALO + tile, :] = u[:, i * LANES:(i + 1) * LANES]
            carry_ref[slab] = u[tile - HALO:, i * LANES:(i + 1) * LANES]

    first = HALO - (CONV_WIDTH - 1)

    def slab_body(s, carry):
        w = dww_ref[s]
        bias = dwb_ref[s]
        for r in range(tile // CONV_ROWS):
            r0 = r * CONV_ROWS
            acc = jnp.broadcast_to(bias, (CONV_ROWS, LANES))
            for k in range(CONV_WIDTH):
                acc = acc + ubuf_ref[s, pl.ds(first + k + r0, CONV_ROWS), :] * w[k:k + 1, :]
            c_ref[s, r0:r0 + CONV_ROWS, :] = acc
        return carry

    lax.fori_loop(0, n_slabs, slab_body, 0)

    c = jnp.concatenate([c_ref[s] for s in range(n_slabs)], axis=-1)
    mu = jnp.mean(c, axis=-1, keepdims=True)
    d = c - mu
    var = jnp.mean(d * d, axis=-1, keepdims=True)
    cn = d * lax.rsqrt(var + EPS) * lng_ref[...] + lnb_ref[...]
    y = (cn * _sigmoid(cn)) * z_ref[...]
    o_ref[0] = x + _dot(y.astype(_BF16), wout_ref[...])


def _const_spec(shape):
    nd = len(shape)
    return pl.BlockSpec(shape, lambda b, t: (0,) * nd, pipeline_mode=pl.Buffered(1))


def _conv_layer(x, norm_w, w_in, dw_w, dw_b, ln_g, ln_b, w_out):
    bsz, seq, _ = x.shape
    tile = min(SEQ_TILE, seq)
    n_slabs = D_INNER // LANES
    dww = dw_w.reshape(CONV_WIDTH, n_slabs, LANES).transpose(1, 0, 2)
    dwb = dw_b.reshape(n_slabs, 1, LANES)
    x_spec = pl.BlockSpec((1, tile, D_MODEL), lambda b, t: (b, t, 0))
    return pl.pallas_call(
        _conv_layer_kernel,
        out_shape=jax.ShapeDtypeStruct(x.shape, x.dtype),
        grid=(bsz, seq // tile),
        in_specs=[
            x_spec,
            _const_spec((1, D_MODEL)),
            _const_spec((D_MODEL, 3 * D_INNER)),
            _const_spec((n_slabs, CONV_WIDTH, LANES)),
            _const_spec((n_slabs, 1, LANES)),
            _const_spec((1, D_INNER)),
            _const_spec((1, D_INNER)),
            _const_spec((D_INNER, D_MODEL)),
        ],
        out_specs=x_spec,
        scratch_shapes=[
            pltpu.VMEM((tile, D_MODEL), _BF16),
            pltpu.VMEM((n_slabs, HALO + tile, LANES), _F32),
            pltpu.VMEM((n_slabs, HALO, LANES), _F32),
            pltpu.VMEM((n_slabs, tile, LANES), _F32),
            pltpu.VMEM((tile, D_INNER), _F32),
        ],
        compiler_params=pltpu.CompilerParams(
            dimension_semantics=("arbitrary", "arbitrary"),
            vmem_limit_bytes=VMEM_LIMIT),
        name="conv_layer",
    )(x, norm_w.reshape(1, D_MODEL), w_in.astype(_BF16), dww, dwb,
      ln_g.reshape(1, D_INNER), ln_b.reshape(1, D_INNER), w_out.astype(_BF16))


def _rope_kernel(pos_ref, freq_ref, cos_ref, sin_ref):
    ang = pos_ref[...].astype(_F32) * freq_ref[...]
    cos_ref[...] = jnp.cos(ang)
    sin_ref[...] = jnp.sin(ang)


def _rope_tables(positions):
    n = positions.size
    half = QK_DIM // 2
    rows = min(1024, n)
    inv_freq = ROPE_BASE ** (-jnp.arange(half, dtype=_F32) / half)
    out = jax.ShapeDtypeStruct((n, half), _F32)
    return pl.pallas_call(
        _rope_kernel,
        out_shape=(out, out),
        grid=(n // rows,),
        in_specs=[pl.BlockSpec((rows, 1), lambda i: (i, 0)),
                  pl.BlockSpec((1, half), lambda i: (0, 0))],
        out_specs=(pl.BlockSpec((rows, half), lambda i: (i, 0)),
                   pl.BlockSpec((rows, half), lambda i: (i, 0))),
        compiler_params=pltpu.CompilerParams(dimension_semantics=("arbitrary",)),
        name="rope_tables",
    )(positions.reshape(n, 1), inv_freq.reshape(1, half))


def _retention_layer_kernel(x_ref, cos_ref, sin_ref, nw_ref, win_ref, wout_ref, dmask_ref,
                            qdec_ref, kdec_ref, cdec_ref, fw_ref,
                            o_ref, hb_ref, q_ref, k_ref, v_ref, y_ref, state_ref, *, final_norm):
    tile = x_ref.shape[1]
    half = QK_DIM // 2

    @pl.when(pl.program_id(1) == 0)
    def _():
        state_ref[...] = jnp.zeros_like(state_ref)

    x = x_ref[0]
    hb_ref[...] = _rms_norm_rows(x, nw_ref[...]).astype(_BF16)
    cos = cos_ref[0]
    sin = sin_ref[0]

    for h in range(HEADS):
        c0 = h * QK_DIM
        hb = hb_ref[...]
        for dst, base, scale in ((q_ref, 0, None), (k_ref, QK_ALL, QK_DIM ** -0.5)):
            t = _dot(hb, win_ref[:, base + c0:base + c0 + QK_DIM])
            t1, t2 = t[:, :half], t[:, half:]
            r1 = t1 * cos - t2 * sin
            r2 = t1 * sin + t2 * cos
            if scale is not None:
                r1, r2 = r1 * scale, r2 * scale
            dst[:, c0:c0 + half] = r1
            dst[:, c0 + half:c0 + QK_DIM] = r2

    for h in range(HEADS):
        c0 = 2 * QK_ALL + h * V_DIM
        v_ref[:, h * V_DIM:(h + 1) * V_DIM] = _dot(hb_ref[...], win_ref[:, c0:c0 + V_DIM]).astype(_BF16)

    for c in range(tile // CHUNK):
        rows = slice(c * CHUNK, (c + 1) * CHUNK)
        for h in range(HEADS):
            q = q_ref[rows, h * QK_DIM:(h + 1) * QK_DIM]
            k = k_ref[rows, h * QK_DIM:(h + 1) * QK_DIM]
            v = v_ref[rows, h * V_DIM:(h + 1) * V_DIM]
            scores = lax.dot_general(q.astype(_BF16), k.astype(_BF16), (((1,), (1,)), ((), ())),
                                     preferred_element_type=_F32) * dmask_ref[h]
            inner = _dot(scores.astype(_BF16), v)
            state = state_ref[h]
            cross = _dot((q * qdec_ref[h]).astype(_BF16), state.astype(_BF16))
            kd = (k * kdec_ref[h]).astype(_BF16)
            state_ref[h] = state * cdec_ref[h] + lax.dot_general(
                kd, v, (((0,), (0,)), ((), ())), preferred_element_type=_F32)
            o = inner + cross
            o = o * lax.rsqrt(jnp.mean(o * o, axis=-1, keepdims=True) + EPS)
            g0 = 2 * QK_ALL + D_INNER + h * V_DIM
            g = _dot(hb_ref[rows, :], win_ref[:, g0:g0 + V_DIM])
            y_ref[rows, h * V_DIM:(h + 1) * V_DIM] = ((g * _sigmoid(g)) * o).astype(_BF16)

    out = x + _dot(y_ref[...], wout_ref[...])
    if final_norm:
        out = _rms_norm_rows(out, fw_ref[...])
    o_ref[0] = out


def _retention_consts():
    log_gamma = jnp.log(1.0 - 2.0 ** (-5.0 - jnp.arange(HEADS, dtype=_F32)))
    idx = jnp.arange(CHUNK, dtype=_F32)
    diff = idx[:, None] - idx[None, :]
    causal = diff >= 0
    dmask = jnp.where(causal[None], jnp.exp(jnp.where(causal, diff, 0.0)[None] * log_gamma[:, None, None]), 0.0)
    qdec = jnp.exp((idx + 1.0)[None, :] * log_gamma[:, None])[..., None]
    kdec = jnp.exp((CHUNK - 1.0 - idx)[None, :] * log_gamma[:, None])[..., None]
    cdec = jnp.broadcast_to(jnp.exp(CHUNK * log_gamma)[:, None, None], (HEADS, 1, V_DIM))
    return dmask, qdec, kdec, cdec


def _retention_layer(x, cos, sin, norm_w, w_in, w_out, final_w, final_norm):
    bsz, seq, _ = x.shape
    tile = min(SEQ_TILE, seq)
    half = QK_DIM // 2
    dmask, qdec, kdec, cdec = _retention_consts()
    x_spec = pl.BlockSpec((1, tile, D_MODEL), lambda b, t: (b, t, 0))
    rope_spec = pl.BlockSpec((1, tile, half), lambda b, t: (b, t, 0))
    return pl.pallas_call(
        functools.partial(_retention_layer_kernel, final_norm=final_norm),
        out_shape=jax.ShapeDtypeStruct(x.shape, x.dtype),
        grid=(bsz, seq // tile),
        in_specs=[
            x_spec, rope_spec, rope_spec,
            _const_spec((1, D_MODEL)),
            _const_spec((D_MODEL, 2 * QK_ALL + 2 * D_INNER)),
            _const_spec((D_INNER, D_MODEL)),
            _const_spec((HEADS, CHUNK, CHUNK)),
            _const_spec((HEADS, CHUNK, 1)),
            _const_spec((HEADS, CHUNK, 1)),
            _const_spec((HEADS, 1, V_DIM)),
            _const_spec((1, D_MODEL)),
        ],
        out_specs=x_spec,
        scratch_shapes=[
            pltpu.VMEM((tile, D_MODEL), _BF16),
            pltpu.VMEM((tile, QK_ALL), _F32),
            pltpu.VMEM((tile, QK_ALL), _F32),
            pltpu.VMEM((tile, D_INNER), _BF16),
            pltpu.VMEM((tile, D_INNER), _BF16),
            pltpu.VMEM((HEADS, QK_DIM, V_DIM), _F32),
        ],
        compiler_params=pltpu.CompilerParams(
            dimension_semantics=("arbitrary", "arbitrary"),
            vmem_limit_bytes=VMEM_LIMIT),
        name="retention_layer",
    )(x, cos, sin, norm_w.reshape(1, D_MODEL), w_in.astype(_BF16), w_out.astype(_BF16),
      dmask, qdec, kdec, cdec, final_w.reshape(1, D_MODEL))


def kernel(x, positions, conv_norm, conv_w_in, conv_dw_w, conv_dw_b, conv_ln_g, conv_ln_b, conv_w_out,
           ret_norm, ret_w_in, ret_w_out, final_norm):
    bsz, seq, _ = x.shape
    cos, sin = _rope_tables(positions)
    cos = cos.reshape(bsz, seq, QK_DIM // 2)
    sin = sin.reshape(bsz, seq, QK_DIM // 2)
    for i in range(DEPTH):
        j = i // 2
        if i % 2 == 0:
            x = _conv_layer(x, conv_norm[j], conv_w_in[j], conv_dw_w[j], conv_dw_b[j],
                            conv_ln_g[j], conv_ln_b[j], conv_w_out[j])
        else:
            x = _retention_layer(x, cos, sin, ret_norm[j], ret_w_in[j], ret_w_out[j],
                                 final_norm, final_norm=(i == DEPTH - 1))
    return x
```
